```python
import jax, jax.numpy as jnp
from jax import lax
import numpy as np

D_MODEL = 4096
BATCH = 4
SEQ = 2048
DEPTH = 2
DEC_BATCH = 8
DEC_SEQ = 1
PAST_LEN = 16384
PAGE_SIZE = 128

HEAD_DIM = 128
SB_HEADS = D_MODEL // (2 * HEAD_DIM)
FOX_HEADS = D_MODEL // (2 * HEAD_DIM)
SB_WIDTH = SB_HEADS * HEAD_DIM
FOX_WIDTH = FOX_HEADS * HEAD_DIM
ATTN_WIDTH = SB_WIDTH + FOX_WIDTH
ATTN_IN = 3 * SB_WIDTH + 3 * FOX_WIDTH + FOX_HEADS + ATTN_WIDTH
Q_BLOCK = 128
FORGET_BIAS_INIT = 2.0

SSD_EXPAND = 2
D_INNER = SSD_EXPAND * D_MODEL
SSD_HEAD_DIM = 64
SSD_HEADS = D_INNER // SSD_HEAD_DIM
SSD_GROUPS = 8
HEADS_PER_GROUP = SSD_HEADS // SSD_GROUPS
D_STATE = 128
D_CONV = 4
CONV_DIM = D_INNER + 2 * SSD_GROUPS * D_STATE
SSD_IN = D_INNER + CONV_DIM + SSD_HEADS
SSD_CHUNK = 128

N_ATTN_LAYERS = (DEPTH + 1) // 2
N_SSD_LAYERS = DEPTH // 2
RMS_EPS = 1e-6

kernel_name = "sb_fox_ssd_hybrid_step"


def _rmsnorm(x, w):
    xf = x.astype(jnp.float32)
    y = xf * lax.rsqrt(jnp.mean(xf * xf, axis=-1, keepdims=True) + RMS_EPS)
    return (y * w.astype(jnp.float32)).astype(x.dtype)


def _block_size(n, blk):
    return blk if n % blk == 0 else n


def _query_blocks(fn, qpos, *qarrs):
    t = qpos.shape[0]
    qb = _block_size(t, Q_BLOCK)
    nb = t // qb
    pos_b = qpos.reshape(nb, qb)
    arrs_b = tuple(jnp.swapaxes(a.reshape(a.shape[0], nb, qb, *a.shape[2:]), 0, 1) for a in qarrs)
    out = lax.map(lambda args: fn(*args), (pos_b,) + arrs_b)
    out = jnp.swapaxes(out, 0, 1)
    return out.reshape(out.shape[0], t, *out.shape[3:])


def _stick_breaking_block(q, k, v, qpos, kpos):
    z = jnp.einsum("bqhd,bshd->bhqs", q, k, preferred_element_type=jnp.float32) * (HEAD_DIM ** -0.5)
    mask = kpos[None, :] < qpos[:, None]
    log_keep = jnp.where(mask, jax.nn.log_sigmoid(-z), 0.0)
    later = lax.cumsum(log_keep, axis=3, reverse=True) - log_keep
    w = jnp.where(mask, jnp.exp(jax.nn.log_sigmoid(z) + later), 0.0)
    o = jnp.einsum("bhqs,bshd->bqhd", w.astype(v.dtype), v, preferred_element_type=jnp.float32)
    return o.astype(v.dtype)


def _forgetting_block(q, cq, k, v, ck_t, qpos, kpos):
    s = jnp.einsum("bqhd,bshd->bhqs", q, k, preferred_element_type=jnp.float32) * (HEAD_DIM ** -0.5)
    logits = s + jnp.swapaxes(cq, 1, 2)[..., None] - ck_t
    mask = kpos[None, :] <= qpos[:, None]
    p = jax.nn.softmax(jnp.where(mask, logits, -jnp.inf), axis=-1)
    o = jnp.einsum("bhqs,bshd->bqhd", p.astype(v.dtype), v, preferred_element_type=jnp.float32)
    return o.astype(v.dtype)


def _hybrid_attention(h, past_kv_sb, past_kv_fox, past_logf, w_in, b_f, w_out):
    bsz, t, _ = h.shape
    past_len = past_kv_sb.shape[1]
    proj = h @ w_in
    o1 = SB_WIDTH; o2 = 2 * SB_WIDTH; o3 = 3 * SB_WIDTH
    o4 = o3 + FOX_WIDTH; o5 = o4 + FOX_WIDTH; o6 = o5 + FOX_WIDTH; o7 = o6 + FOX_HEADS
    q_sb, k_sb, v_sb, q_fx, k_fx, v_fx, f_logit, gate = jnp.split(proj, [o1, o2, o3, o4, o5, o6, o7], axis=-1)
    q_sb = q_sb.reshape(bsz, t, SB_HEADS, HEAD_DIM)
    q_fx = q_fx.reshape(bsz, t, FOX_HEADS, HEAD_DIM)
    kv_sb = jnp.stack([k_sb.reshape(bsz, t, SB_HEADS, HEAD_DIM), v_sb.reshape(bsz, t, SB_HEADS, HEAD_DIM)], axis=2)
    kv_fx = jnp.stack([k_fx.reshape(bsz, t, FOX_HEADS, HEAD_DIM), v_fx.reshape(bsz, t, FOX_HEADS, HEAD_DIM)], axis=2)
    logf = jax.nn.log_sigmoid(f_logit.astype(jnp.float32) + b_f.astype(jnp.float32))

    all_sb = jnp.concatenate([past_kv_sb.astype(kv_sb.dtype), kv_sb], axis=1)
    all_fx = jnp.concatenate([past_kv_fox.astype(kv_fx.dtype), kv_fx], axis=1)
    k_sb_all, v_sb_all = all_sb[:, :, 0], all_sb[:, :, 1]
    k_fx_all, v_fx_all = all_fx[:, :, 0], all_fx[:, :, 1]
    qpos = past_len + jnp.arange(t, dtype=jnp.int32)
    kpos = jnp.arange(past_len + t, dtype=jnp.int32)
    csum = jnp.cumsum(jnp.concatenate([past_logf.astype(jnp.float32), logf], axis=1), axis=1)
    cq = csum[:, past_len:]
    ck_t = jnp.swapaxes(csum, 1, 2)[:, :, None, :]

    o_sb = _query_blocks(lambda pos, qb: _stick_breaking_block(qb, k_sb_all, v_sb_all, pos, kpos), qpos, q_sb)
    o_fx = _query_blocks(lambda pos, qb, cqb: _forgetting_block(qb, cqb, k_fx_all, v_fx_all, ck_t, pos, kpos),
                         qpos, q_fx, cq)
    mixed = jnp.concatenate([o_sb.reshape(bsz, t, SB_WIDTH), o_fx.reshape(bsz, t, FOX_WIDTH)], axis=-1)
    mixed = mixed * jax.nn.silu(gate)
    return mixed @ w_out, kv_sb, kv_fx, logf


def _segsum_exp(a):
    n = a.shape[-1]
    cs = jnp.cumsum(a, axis=-1)
    diff = cs[..., :, None] - cs[..., None, :]
    tril = jnp.tril(jnp.ones((n, n), dtype=bool))
    return jnp.exp(jnp.where(tril, diff, -jnp.inf))


def _ssd_scan(x, da, b_in, c_in, h0):
    bsz, seqlen = x.shape[:2]
    q = _block_size(seqlen, SSD_CHUNK)
    nc = seqlen // q
    xc = x.reshape(bsz, nc, q, SSD_GROUPS, HEADS_PER_GROUP, SSD_HEAD_DIM)
    bc = b_in.reshape(bsz, nc, q, SSD_GROUPS, D_STATE)
    cc = c_in.reshape(bsz, nc, q, SSD_GROUPS, D_STATE)
    a = jnp.moveaxis(da.reshape(bsz, nc, q, SSD_GROUPS, HEADS_PER_GROUP), (3, 4), (1, 2))
    a_cs = jnp.cumsum(a, axis=-1)
    decay_in = _segsum_exp(a)
    cb = jnp.einsum("bclgn,bcsgn->bcgls", cc, bc)
    y_diag = jnp.einsum("bcgls,bgecls,bcsgep->bclgep", cb, decay_in, xc)
    decay_states = jnp.exp(a_cs[..., -1:] - a_cs)
    states = jnp.einsum("bclgn,bgecl,bclgep->bcgepn", bc, decay_states, xc)
    states = jnp.concatenate([h0[:, None], states], axis=1)
    chunk_decay = _segsum_exp(jnp.pad(a_cs[..., -1], ((0, 0), (0, 0), (0, 0), (1, 0))))
    states = jnp.einsum("bgezc,bcgepn->bzgepn", chunk_decay, states)
    y_off = jnp.einsum("bclgn,bcgepn,bgecl->bclgep", cc, states[:, :-1], jnp.exp(a_cs))
    y = (y_diag + y_off).reshape(bsz, seqlen, SSD_GROUPS, HEADS_PER_GROUP, SSD_HEAD_DIM)
    return y, states[:, -1]


def _ssd_mixer(h, conv_prev, ssm_prev, w_in, conv_w, conv_b, dt_bias, a_log, d_skip, norm_w, w_out):
    bsz, t, _ = h.shape
    proj = h @ w_in
    z, xbc, dt_raw = jnp.split(proj, [D_INNER, D_INNER + CONV_DIM], axis=-1)
    full = jnp.concatenate([conv_prev.astype(xbc.dtype), xbc], axis=1)
    conv = conv_b
    for j in range(D_CONV):
        conv = conv + full[:, j:j + t] * conv_w[j]
    xbc_c = jax.nn.silu(conv)
    xs, b_in, c_in = jnp.split(xbc_c, [D_INNER, D_INNER + SSD_GROUPS * D_STATE], axis=-1)
    dt = jax.nn.softplus(dt_raw.astype(jnp.float32) + dt_bias.astype(jnp.float32))
    a = -jnp.exp(a_log.astype(jnp.float32)).reshape(SSD_GROUPS, HEADS_PER_GROUP)
    dtg = dt.reshape(bsz, t, SSD_GROUPS, HEADS_PER_GROUP)
    xh = xs.astype(jnp.float32).reshape(bsz, t, SSD_GROUPS, HEADS_PER_GROUP, SSD_HEAD_DIM)
    h0 = ssm_prev.astype(jnp.float32).reshape(bsz, SSD_GROUPS, HEADS_PER_GROUP, SSD_HEAD_DIM, D_STATE)
    y, h_last = _ssd_scan(xh * dtg[..., None], dtg * a,
                          b_in.astype(jnp.float32).reshape(bsz, t, SSD_GROUPS, D_STATE),
                          c_in.astype(jnp.float32).reshape(bsz, t, SSD_GROUPS, D_STATE), h0)
    y = y + xh * d_skip.astype(jnp.float32).reshape(SSD_GROUPS, HEADS_PER_GROUP, 1)
    y = y.reshape(bsz, t, D_INNER) * jax.nn.silu(z.astype(jnp.float32))
    y = _rmsnorm(y, norm_w).astype(h.dtype)
    new_conv = full[:, full.shape[1] - (D_CONV - 1):]
    new_ssm = h_last.reshape(bsz, SSD_HEADS, SSD_HEAD_DIM, D_STATE).astype(ssm_prev.dtype)
    return y @ w_out, new_conv, new_ssm


def _forward(x, attn_past, ssd_past, norm_pre, norm_post, attn_w_in, attn_b_f, attn_w_out,
             ssd_w_in, ssd_conv_w, ssd_conv_b, ssd_dt_bias, ssd_A_log, ssd_D, ssd_norm_w, ssd_w_out):
    attn_new, ssd_new = [], []
    for layer in range(DEPTH):
        i = layer // 2
        h = _rmsnorm(x, norm_pre[layer])
        if layer % 2 == 0:
            out, kv_sb, kv_fx, logf = _hybrid_attention(h, attn_past[i][0], attn_past[i][1], attn_past[i][2],
                                                        attn_w_in[i], attn_b_f[i], attn_w_out[i])
            attn_new.append((kv_sb, kv_fx, logf))
        else:
            out, conv_s, ssm_s = _ssd_mixer(h, ssd_past[i][0], ssd_past[i][1], ssd_w_in[i], ssd_conv_w[i],
                                            ssd_conv_b[i], ssd_dt_bias[i], ssd_A_log[i], ssd_D[i],
                                            ssd_norm_w[i], ssd_w_out[i])
            ssd_new.append((conv_s, ssm_s))
        x = x + _rmsnorm(out, norm_post[layer])
    return x, attn_new, ssd_new


def setup_inputs(seed: int = 0) -> dict:
    key = jax.random.key(seed)
    ks = jax.random.split(key, 24)
    n_pages = PAST_LEN // PAGE_SIZE
    n_used = DEC_BATCH * n_pages
    n_pool = n_used + n_used // 4
    f32 = jnp.float32

    def nrm(k, shape, scale):
        return jax.random.normal(k, shape, f32) * scale

    page_table = jax.random.permutation(ks[0], n_pool)[:n_used].reshape(DEC_BATCH, n_pages).astype(jnp.int32)
    dt0 = jnp.exp(jax.random.uniform(ks[1], (N_SSD_LAYERS, SSD_HEADS), f32) * (np.log(0.1) - np.log(0.001))
                  + np.log(0.001))
    return {
        "x_prompt": nrm(ks[2], (BATCH, SEQ, D_MODEL), 1.0),
        "x_sample": nrm(ks[3], (DEC_BATCH, DEC_SEQ, D_MODEL), 1.0),
        "cache_kv_sb": nrm(ks[4], (N_ATTN_LAYERS, n_pool, PAGE_SIZE, 2, SB_HEADS, HEAD_DIM), 1.0),
        "cache_kv_fox": nrm(ks[5], (N_ATTN_LAYERS, n_pool, PAGE_SIZE, 2, FOX_HEADS, HEAD_DIM), 1.0),
        "cache_logf_fox": jax.nn.log_sigmoid(FORGET_BIAS_INIT + nrm(ks[6], (N_ATTN_LAYERS, n_pool, PAGE_SIZE, FOX_HEADS), 1.0)),
        "state_conv": nrm(ks[7], (N_SSD_LAYERS, DEC_BATCH, D_CONV - 1, CONV_DIM), 1.0),
        "state_ssm": nrm(ks[8], (N_SSD_LAYERS, DEC_BATCH, SSD_HEADS, SSD_HEAD_DIM, D_STATE), 0.1),
        "page_table": page_table,
        "norm_pre": 1.0 + nrm(ks[9], (DEPTH, D_MODEL), 0.02),
        "norm_post": 1.0 + nrm(ks[10], (DEPTH, D_MODEL), 0.02),
        "attn_w_in": nrm(ks[11], (N_ATTN_LAYERS, D_MODEL, ATTN_IN), D_MODEL ** -0.5),
        "attn_b_f": FORGET_BIAS_INIT + nrm(ks[12], (N_ATTN_LAYERS, FOX_HEADS), 0.1),
        "attn_w_out": nrm(ks[13], (N_ATTN_LAYERS, ATTN_WIDTH, D_MODEL), ATTN_WIDTH ** -0.5),
        "ssd_w_in": nrm(ks[14], (N_SSD_LAYERS, D_MODEL, SSD_IN), D_MODEL ** -0.5),
        "ssd_conv_w": nrm(ks[15], (N_SSD_LAYERS, D_CONV, CONV_DIM), D_CONV ** -0.5),
        "ssd_conv_b": nrm(ks[16], (N_SSD_LAYERS, CONV_DIM), 0.02),
        "ssd_dt_bias": dt0 + jnp.log(-jnp.expm1(-dt0)),
        "ssd_A_log": jnp.log(jax.random.uniform(ks[17], (N_SSD_LAYERS, SSD_HEADS), f32, 1.0, 16.0)),
        "ssd_D": 1.0 + nrm(ks[18], (N_SSD_LAYERS, SSD_HEADS), 0.1),
        "ssd_norm_w": 1.0 + nrm(ks[19], (N_SSD_LAYERS, D_INNER), 0.02),
        "ssd_w_out": nrm(ks[20], (N_SSD_LAYERS, D_INNER, D_MODEL), D_INNER ** -0.5),
    }


def reference(x_prompt, x_sample, cache_kv_sb, cache_kv_fox, cache_logf_fox, state_conv, state_ssm, page_table,
              norm_pre, norm_post, attn_w_in, attn_b_f, attn_w_out, ssd_w_in, ssd_conv_w, ssd_conv_b,
              ssd_dt_bias, ssd_A_log, ssd_D, ssd_norm_w, ssd_w_out):
    weights = (norm_pre, norm_post, attn_w_in, attn_b_f, attn_w_out, ssd_w_in, ssd_conv_w, ssd_conv_b,
               ssd_dt_bias, ssd_A_log, ssd_D, ssd_norm_w, ssd_w_out)
    bp = x_prompt.shape[0]
    db = x_sample.shape[0]

    attn_past_p = [(jnp.zeros((bp, 0, 2, SB_HEADS, HEAD_DIM), x_prompt.dtype),
                    jnp.zeros((bp, 0, 2, FOX_HEADS, HEAD_DIM), x_prompt.dtype),
                    jnp.zeros((bp, 0, FOX_HEADS), jnp.float32)) for _ in range(N_ATTN_LAYERS)]
    ssd_past_p = [(jnp.zeros((bp, D_CONV - 1, CONV_DIM), x_prompt.dtype),
                   jnp.zeros((bp, SSD_HEADS, SSD_HEAD_DIM, D_STATE), jnp.float32)) for _ in range(N_SSD_LAYERS)]
    y_prompt, attn_p, ssd_p = _forward(x_prompt, attn_past_p, ssd_past_p, *weights)

    attn_past_s = [(cache_kv_sb[i, page_table].reshape(db, -1, 2, SB_HEADS, HEAD_DIM),
                    cache_kv_fox[i, page_table].reshape(db, -1, 2, FOX_HEADS, HEAD_DIM),
                    cache_logf_fox[i, page_table].reshape(db, -1, FOX_HEADS)) for i in range(N_ATTN_LAYERS)]
    ssd_past_s = [(state_conv[i], state_ssm[i]) for i in range(N_SSD_LAYERS)]
    y_sample, attn_s, ssd_s = _forward(x_sample, attn_past_s, ssd_past_s, *weights)

    kv_sb_prompt = jnp.stack([a[0] for a in attn_p])
    kv_sb_sample = jnp.stack([a[0] for a in attn_s])
    kv_fox_prompt = jnp.stack([a[1] for a in attn_p])
    kv_fox_sample = jnp.stack([a[1] for a in attn_s])
    logf_prompt = jnp.stack([a[2] for a in attn_p])
    logf_sample = jnp.stack([a[2] for a in attn_s])
    conv_prompt = jnp.stack([s[0] for s in ssd_p])
    conv_sample = jnp.stack([s[0] for s in ssd_s])
    ssm_prompt = jnp.stack([s[1] for s in ssd_p])
    ssm_sample = jnp.stack([s[1] for s in ssd_s])
    return (y_prompt, y_sample, kv_sb_prompt, kv_sb_sample, kv_fox_prompt, kv_fox_sample,
            logf_prompt, logf_sample, conv_prompt, conv_sample, ssm_prompt, ssm_sample)
```

```python
import functools

import jax
import jax.numpy as jnp
from jax import lax
from jax.experimental import pallas as pl
from jax.experimental.pallas import tpu as pltpu

F32 = jnp.float32
BF16 = jnp.bfloat16
HIGHEST = lax.Precision.HIGHEST

HEAD_DIM = 128
N_HEADS = 16
ATTN_SCALE = HEAD_DIM ** -0.5
SSD_HEAD_DIM = 64
SSD_GROUPS = 8
SSD_HEADS_PER_GROUP = 16
SSD_STATE = 128
SSD_CHUNK = 128
D_CONV = 4
RMS_EPS = 1e-6
LANES = 128
SUBLANES = 8
VMEM_LIMIT = 56 * 1024 * 1024
NEG_BIG = -1e30

_NT = (((1,), (1,)), ((), ()))


def _params(*sem):
    return pltpu.CompilerParams(dimension_semantics=sem, vmem_limit_bytes=VMEM_LIMIT)


def _softplus(x):
    return jnp.maximum(x, 0.0) + jnp.log(1.0 + jnp.exp(-jnp.abs(x)))


def _log_sigmoid(x):
    return jnp.minimum(x, 0.0) - jnp.log(1.0 + jnp.exp(-jnp.abs(x)))


def _silu(x):
    return x * (1.0 / (1.0 + jnp.exp(-x)))


def _split2(x):
    hi = x.astype(BF16)
    lo = (x - hi.astype(F32)).astype(BF16)
    return hi, lo


def _iota(shape, dim):
    return lax.broadcasted_iota(jnp.int32, shape, dim)


def _rmsnorm_kernel(x_ref, w_ref, o_ref):
    x = x_ref[...]
    ms = jnp.mean(x * x, axis=-1, keepdims=True)
    o_ref[...] = (x * lax.rsqrt(ms + RMS_EPS) * w_ref[...]).astype(o_ref.dtype)


def rmsnorm_bf16(x, w):
    m, d = x.shape
    tm = min(m, 256)
    return pl.pallas_call(
        _rmsnorm_kernel,
        out_shape=jax.ShapeDtypeStruct((m, d), BF16),
        grid=(m // tm,),
        in_specs=[pl.BlockSpec((tm, d), lambda i: (i, 0)),
                  pl.BlockSpec((1, d), lambda i: (0, 0))],
        out_specs=pl.BlockSpec((tm, d), lambda i: (i, 0)),
        compiler_params=_params("parallel"),
        name="rmsnorm_bf16",
    )(x, w.reshape(1, d))


def _matmul_kernel(a_ref, w_ref, o_ref):
    o_ref[...] = jnp.dot(a_ref[...], w_ref[...], preferred_element_type=F32)


def matmul(a, w):
    m, k = a.shape
    n = w.shape[1]
    tm = min(m, 1024)
    tn = 512 if n % 512 == 0 else LANES
    return pl.pallas_call(
        _matmul_kernel,
        out_shape=jax.ShapeDtypeStruct((m, n), F32),
        grid=(m // tm, n // tn),
        in_specs=[pl.BlockSpec((tm, k), lambda i, j: (i, 0)),
                  pl.BlockSpec((k, tn), lambda i, j: (0, j))],
        out_specs=pl.BlockSpec((tm, tn), lambda i, j: (i, j)),
        compiler_params=_params("parallel", "arbitrary"),
        name="matmul",
    )(a, w)


def _proj_norm_res_kernel(a_ref, w_ref, nw_ref, x_ref, o_ref):
    k = pl.program_id(1)

    @pl.when(k == 0)
    def _():
        o_ref[...] = jnp.zeros_like(o_ref)

    o_ref[...] += jnp.dot(a_ref[...], w_ref[...], preferred_element_type=F32)

    @pl.when(k == pl.num_programs(1) - 1)
    def _():
        r = o_ref[...]
        ms = jnp.mean(r * r, axis=-1, keepdims=True)
        o_ref[...] = x_ref[...] + r * lax.rsqrt(ms + RMS_EPS) * nw_ref[...]


def proj_norm_residual(a, w, nw, x):
    m, kdim = a.shape
    d = w.shape[1]
    tm = min(m, 512)
    tk = 512
    return pl.pallas_call(
        _proj_norm_res_kernel,
        out_shape=jax.ShapeDtypeStruct((m, d), F32),
        grid=(m // tm, kdim // tk),
        in_specs=[pl.BlockSpec((tm, tk), lambda i, k: (i, k)),
                  pl.BlockSpec((tk, d), lambda i, k: (k, 0)),
                  pl.BlockSpec((1, d), lambda i, k: (0, 0)),
                  pl.BlockSpec((tm, d), lambda i, k: (i, 0))],
        out_specs=pl.BlockSpec((tm, d), lambda i, k: (i, 0)),
        compiler_params=_params("parallel", "arbitrary"),
        name="proj_norm_residual",
    )(a, w, nw.reshape(1, d), x)


def _logf_kernel(f_ref, b_ref, logf_ref, csum_ref, csumt_ref):
    t = f_ref.shape[1]
    tri = (_iota((LANES, LANES), 1) <= _iota((LANES, LANES), 0)).astype(F32)
    carry = jnp.zeros((1, LANES), F32)
    for c in range(t // LANES):
        rows = pl.ds(c * LANES, LANES)
        lf = _log_sigmoid(f_ref[0, rows, :] + b_ref[...])
        logf_ref[0, rows, :] = lf
        cs = jnp.dot(tri, lf, precision=HIGHEST, preferred_element_type=F32) + carry
        csum_ref[0, rows, :] = cs
        csumt_ref[0, :, rows] = cs.T
        carry = cs[LANES - 1:LANES, :]


def logf_cumsum(f, b):
    bsz, t, w = f.shape
    blk = pl.BlockSpec((1, t, w), lambda i: (i, 0, 0))
    return pl.pallas_call(
        _logf_kernel,
        out_shape=(jax.ShapeDtypeStruct((bsz, t, w), F32),
                   jax.ShapeDtypeStruct((bsz, t, w), F32),
                   jax.ShapeDtypeStruct((bsz, w, t), F32)),
        grid=(bsz,),
        in_specs=[blk, pl.BlockSpec((1, w), lambda i: (0, 0))],
        out_specs=(blk, blk, pl.BlockSpec((1, w, t), lambda i: (i, 0, 0))),
        compiler_params=_params("parallel"),
        name="logf_cumsum",
    )(f, b)


ATT_TQ = 256
ATT_TK = 128


def _attn_kernel(q_ref, ksb_ref, vsb_ref, kfx_ref, vfx_ref, cq_ref, ck_ref, gate_ref, o_ref):
    g = pl.program_id(1)
    qi = pl.program_id(2)
    tq, tk = ATT_TQ, ATT_TK
    nd = tq // tk
    n_off = qi * nd
    q = q_ref[0].astype(BF16)
    qpos = qi * tq + _iota((tq, tk), 0)
    kloc = _iota((tq, tk), 1)

    def finish(o):
        o_ref[0] = (o * _silu(gate_ref[0])).astype(o_ref.dtype)

    @pl.when(g < N_HEADS)
    def _stick_breaking():
        tm2 = ((_iota((2 * tk, tk), 0) % tk) > _iota((2 * tk, tk), 1)).astype(BF16)

        def block(j, carry, acc, masked):
            rows = pl.ds(pl.multiple_of(j * tk, tk), tk)
            kb = ksb_ref[0, rows, :].astype(BF16)
            vb = vsb_ref[0, rows, :].astype(BF16)
            z = lax.dot_general(q, kb, _NT, preferred_element_type=F32) * ATTN_SCALE
            sp = _softplus(z)
            lk = -sp
            lb = z - sp
            if masked:
                mask = (j * tk + kloc) < qpos
                lk = jnp.where(mask, lk, 0.0)
            hi, lo = _split2(lk)
            later = jnp.dot(jnp.concatenate([hi, lo], axis=1), tm2, preferred_element_type=F32)
            w = jnp.exp(lb + later + carry)
            if masked:
                w = jnp.where(mask, w, 0.0)
            acc = acc + jnp.dot(w.astype(BF16), vb, preferred_element_type=F32)
            carry = carry + (later[:, 0:1] + lk[:, 0:1])
            return carry, acc

        carry = jnp.zeros((tq, 1), F32)
        acc = jnp.zeros((tq, HEAD_DIM), F32)
        for d in range(nd):
            carry, acc = block(n_off + (nd - 1 - d), carry, acc, True)

        def body(it, ca):
            return block(n_off - 1 - it, ca[0], ca[1], False)

        carry, acc = lax.fori_loop(0, n_off, body, (carry, acc))
        finish(acc)

    @pl.when(g >= N_HEADS)
    def _forgetting():
        hh = g - N_HEADS
        nk = ck_ref.shape[1] // N_HEADS
        cq = jnp.sum(jnp.where(_iota((tq, LANES), 1) == hh, cq_ref[0], 0.0), axis=1, keepdims=True)

        def block(j, m, l, acc, masked):
            rows = pl.ds(pl.multiple_of(j * tk, tk), tk)
            kb = kfx_ref[0, rows, :].astype(BF16)
            vb = vfx_ref[0, rows, :].astype(BF16)
            s = lax.dot_general(q, kb, _NT, preferred_element_type=F32) * ATTN_SCALE
            s = s + cq - ck_ref[0, hh * nk + j]
            if masked:
                mask = (j * tk + kloc) <= qpos
                s = jnp.where(mask, s, NEG_BIG)
            m_new = jnp.maximum(m, jnp.max(s, axis=1, keepdims=True))
            p = jnp.exp(s - m_new)
            if masked:
                p = jnp.where(mask, p, 0.0)
            alpha = jnp.exp(m - m_new)
            l = alpha * l + jnp.sum(p, axis=1, keepdims=True)
            acc = alpha * acc + jnp.dot(p.astype(BF16), vb, preferred_element_type=F32)
            return m_new, l, acc

        def body(j, c):
            return block(j, c[0], c[1], c[2], False)

        m = jnp.full((tq, 1), NEG_BIG, F32)
        l = jnp.zeros((tq, 1), F32)
        acc = jnp.zeros((tq, HEAD_DIM), F32)
        m, l, acc = lax.fori_loop(0, n_off, body, (m, l, acc))
        for d in range(nd):
            m, l, acc = block(n_off + d, m, l, acc, True)
        finish(acc / l)


def prompt_attention(q, kv_sb, kv_fx, csum, ck, gate):
    bsz, t, width = q.shape
    h = N_HEADS
    tq = ATT_TQ
    sb = lambda g: jnp.minimum(g, h - 1)
    fx = lambda g: jnp.maximum(g - h, 0)
    qspec = pl.BlockSpec((1, tq, HEAD_DIM), lambda b, g, i: (b, i, g))
    return pl.pallas_call(
        _attn_kernel,
        out_shape=jax.ShapeDtypeStruct((bsz, t, width), BF16),
        grid=(bsz, 2 * h, t // tq),
        in_specs=[qspec,
                  pl.BlockSpec((1, t, HEAD_DIM), lambda b, g, i: (b, 0, sb(g))),
                  pl.BlockSpec((1, t, HEAD_DIM), lambda b, g, i: (b, 0, h + sb(g))),
                  pl.BlockSpec((1, t, HEAD_DIM), lambda b, g, i: (b, 0, fx(g))),
                  pl.BlockSpec((1, t, HEAD_DIM), lambda b, g, i: (b, 0, h + fx(g))),
                  pl.BlockSpec((1, tq, LANES), lambda b, g, i: (b, i, 0)),
                  pl.BlockSpec((1,) + ck.shape[1:], lambda b, g, i: (b, 0, 0, 0)),
                  qspec],
        out_specs=qspec,
        compiler_params=_params("parallel", "parallel", "arbitrary"),
        name="prompt_attention",
    )(q, kv_sb, kv_sb, kv_fx, kv_fx, csum, ck, gate)


def _head_onehot_rows(dtype):
    w = N_HEADS * HEAD_DIM
    return ((_iota((LANES, w), 1) // HEAD_DIM) == _iota((LANES, w), 0)).astype(dtype)


def _suffix_matrix():
    return ((_iota((LANES, 2 * LANES), 1) % LANES) > _iota((LANES, 2 * LANES), 0)).astype(BF16)


def _sum_row_groups(x):
    acc = x[0:SUBLANES]
    for r in range(1, x.shape[0] // SUBLANES):
        acc = acc + x[r * SUBLANES:(r + 1) * SUBLANES]
    return acc


def _decode_sb_kernel(pt_ref, q_ref, page_ref, gate_ref, o_ref, qbd_ref, e_ref, u_ref, carry_ref, acc_ref):
    j = pl.program_id(1)
    w = N_HEADS * HEAD_DIM

    @pl.when(j == 0)
    def _():
        e = _head_onehot_rows(F32)
        e_ref[...] = e.astype(BF16)
        qbd_ref[...] = (e * q_ref[0]).astype(BF16)
        u_ref[...] = _suffix_matrix()
        carry_ref[...] = jnp.zeros_like(carry_ref)
        acc_ref[...] = jnp.zeros_like(acc_ref)

    kp = page_ref[0, :, :w].astype(BF16)
    z = lax.dot_general(kp, qbd_ref[...], _NT, preferred_element_type=F32) * ATTN_SCALE
    sp = _softplus(z)
    lk = -sp
    lb = z - sp
    hi, lo = _split2(lk)
    later = jnp.dot(u_ref[...], jnp.concatenate([hi, lo], axis=0), preferred_element_type=F32)
    wgt = jnp.exp(lb + later + carry_ref[...])
    carry_ref[...] += jnp.sum(lk, axis=0, keepdims=True)
    wexp = jnp.dot(wgt.astype(BF16), e_ref[...], preferred_element_type=F32)
    acc_ref[...] += _sum_row_groups(wexp * page_ref[0, :, w:])

    @pl.when(j == pl.num_programs(1) - 1)
    def _():
        o = jnp.sum(acc_ref[...], axis=0, keepdims=True)
        o_ref[0] = (o * _silu(gate_ref[0])).astype(o_ref.dtype)


def _decode_fx_kernel(pt_ref, q_ref, kvn_ref, lfn_ref, page_ref, lf_ref, gate_ref, o_ref,
                      qbd_ref, e_ref, u_ref, lfp_ref, d_ref, m_ref, l_ref, acc_ref):
    j = pl.program_id(1)
    w = N_HEADS * HEAD_DIM

    def expand_rows(v):
        v8 = jnp.broadcast_to(v, (SUBLANES, LANES))
        return jnp.dot(v8, _head_onehot_rows(F32), precision=HIGHEST, preferred_element_type=F32)[0:1]

    @pl.when(j == 0)
    def _():
        e = _head_onehot_rows(F32)
        e_ref[...] = e.astype(BF16)
        qbd_ref[...] = (e * q_ref[0]).astype(BF16)
        u_ref[...] = _suffix_matrix()
        lfp_ref[...] = jnp.zeros_like(lfp_ref)
        k_new = jnp.broadcast_to(kvn_ref[0, :, :w], (SUBLANES, w)).astype(BF16)
        z_new = lax.dot_general(k_new, qbd_ref[...], _NT, preferred_element_type=F32)[0:1] * ATTN_SCALE
        m_ref[...] = z_new
        l_ref[...] = jnp.ones_like(l_ref)
        row0 = (_iota((SUBLANES, w), 0) == 0).astype(F32)
        acc_ref[...] = row0 * kvn_ref[0, :, w:]
        d_ref[...] = lfn_ref[0]

    kp = page_ref[0, :, :w].astype(BF16)
    z = lax.dot_general(kp, qbd_ref[...], _NT, preferred_element_type=F32) * ATTN_SCALE
    lfp_ref[:, 0:N_HEADS] = lf_ref[0]
    lf = lfp_ref[...]
    hi, lo = _split2(lf)
    dec = jnp.dot(u_ref[...], jnp.concatenate([hi, lo], axis=0), preferred_element_type=F32) + d_ref[...]
    d_ref[...] += jnp.sum(lf, axis=0, keepdims=True)
    s = z + dec
    m_old = m_ref[...]
    m_new = jnp.maximum(m_old, jnp.max(s, axis=0, keepdims=True))
    alpha = jnp.exp(m_old - m_new)
    p = jnp.exp(s - m_new)
    l_ref[...] = alpha * l_ref[...] + jnp.sum(p, axis=0, keepdims=True)
    m_ref[...] = m_new
    pexp = jnp.dot(p.astype(BF16), e_ref[...], preferred_element_type=F32)
    acc_ref[...] = acc_ref[...] * expand_rows(alpha) + _sum_row_groups(pexp * page_ref[0, :, w:])

    @pl.when(j == pl.num_programs(1) - 1)
    def _():
        o = jnp.sum(acc_ref[...], axis=0, keepdims=True) / expand_rows(l_ref[...])
        o_ref[0] = (o * _silu(gate_ref[0])).astype(o_ref.dtype)


def decode_attention(page_table, q, kv_fx_new, logf_new, cache_sb, cache_fx, cache_lf, gate):
    db, n_pages = page_table.shape
    page = cache_sb.shape[1]
    w = N_HEADS * HEAD_DIM
    assert page == LANES
    last = n_pages - 1
    page_map = lambda b, j, pt: (pt[b, last - j], 0, 0)
    row = lambda blk: pl.BlockSpec((1, 1, w), lambda b, j, pt: (b, 0, blk))
    common_scratch = [pltpu.VMEM((LANES, w), BF16),
                      pltpu.VMEM((LANES, w), BF16),
                      pltpu.VMEM((LANES, 2 * LANES), BF16)]
    a_sb = pl.pallas_call(
        _decode_sb_kernel,
        out_shape=jax.ShapeDtypeStruct((db, 1, w), BF16),
        grid_spec=pltpu.PrefetchScalarGridSpec(
            num_scalar_prefetch=1,
            grid=(db, n_pages),
            in_specs=[row(0),
                      pl.BlockSpec((1, page, 2 * w), page_map),
                      row(0)],
            out_specs=pl.BlockSpec((1, 1, w), lambda b, j, pt: (b, 0, 0)),
            scratch_shapes=common_scratch + [pltpu.VMEM((1, LANES), F32),
                                             pltpu.VMEM((SUBLANES, w), F32)]),
        compiler_params=_params("parallel", "arbitrary"),
        name="decode_stick_breaking",
    )(page_table, q, cache_sb, gate)
    a_fx = pl.pallas_call(
        _decode_fx_kernel,
        out_shape=jax.ShapeDtypeStruct((db, 1, w), BF16),
        grid_spec=pltpu.PrefetchScalarGridSpec(
            num_scalar_prefetch=1,
            grid=(db, n_pages),
            in_specs=[row(1),
                      pl.BlockSpec((1, 1, 2 * w), lambda b, j, pt: (b, 0, 0)),
                      pl.BlockSpec((1, 1, LANES), lambda b, j, pt: (b, 0, 0)),
                      pl.BlockSpec((1, page, 2 * w), page_map),
                      pl.BlockSpec((1, page, N_HEADS), page_map),
                      row(1)],
            out_specs=pl.BlockSpec((1, 1, w), lambda b, j, pt: (b, 0, 0)),
            scratch_shapes=common_scratch + [pltpu.VMEM((page, LANES), F32),
                                             pltpu.VMEM((1, LANES), F32),
                                             pltpu.VMEM((1, LANES), F32),
                                             pltpu.VMEM((1, LANES), F32),
                                             pltpu.VMEM((SUBLANES, w), F32)]),
        compiler_params=_params("parallel", "arbitrary"),
        name="decode_forgetting",
    )(page_table, q, kv_fx_new, logf_new, cache_fx, cache_lf, gate)
    return a_sb, a_fx


def _logf_new_kernel(f_ref, b_ref, o_ref):
    o_ref[...] = _log_sigmoid(f_ref[...] + b_ref[...])


def logf_new(f, b):
    return pl.pallas_call(
        _logf_new_kernel,
        out_shape=jax.ShapeDtypeStruct(f.shape, F32),
        name="logf_new",
    )(f, b)


CONV_TT = 512
CONV_TC = 1024
CONV_HALO = SUBLANES


def _conv_prompt_kernel(x_ref, prev_ref, w_ref, b_ref, o_ref, buf_ref):
    ti = pl.program_id(1)
    tt = x_ref.shape[1]
    halo = prev_ref[0]
    buf_ref[0:CONV_HALO, :] = jnp.where(ti > 0, halo, jnp.zeros_like(halo))
    buf_ref[CONV_HALO:, :] = x_ref[0]
    acc = jnp.broadcast_to(b_ref[...], (tt, b_ref.shape[1]))
    for j in range(D_CONV):
        start = CONV_HALO - (D_CONV - 1) + j
        acc = acc + buf_ref[pl.ds(start, tt), :] * w_ref[j:j + 1, :]
    o_ref[0] = _silu(acc)


def conv_prompt(xbc, w, b):
    bsz, t, c = xbc.shape
    tt = min(t, CONV_TT)
    tc = CONV_TC
    per = tt // CONV_HALO
    return pl.pallas_call(
        _conv_prompt_kernel,
        out_shape=jax.ShapeDtypeStruct((bsz, t, c), F32),
        grid=(bsz, t // tt, c // tc),
        in_specs=[pl.BlockSpec((1, tt, tc), lambda bb, i, j: (bb, i, j)),
                  pl.BlockSpec((1, CONV_HALO, tc), lambda bb, i, j: (bb, jnp.maximum(i * per - 1, 0), j)),
                  pl.BlockSpec((D_CONV, tc), lambda bb, i, j: (0, j)),
                  pl.BlockSpec((1, tc), lambda bb, i, j: (0, j))],
        out_specs=pl.BlockSpec((1, tt, tc), lambda bb, i, j: (bb, i, j)),
        scratch_shapes=[pltpu.VMEM((tt + CONV_HALO, tc), F32)],
        compiler_params=_params("parallel", "parallel", "parallel"),
        name="conv_prompt",
    )(xbc, xbc, w, b.reshape(1, c))


def _conv_step_kernel(state_ref, x_ref, w_ref, b_ref, o_ref, new_ref):
    acc = b_ref[...] + x_ref[...] * w_ref[D_CONV - 1:D_CONV, :]
    for j in range(D_CONV - 1):
        acc = acc + state_ref[:, j, :] * w_ref[j:j + 1, :]
    o_ref[...] = _silu(acc)
    for j in range(D_CONV - 2):
        new_ref[:, j, :] = state_ref[:, j + 1, :]
    new_ref[:, D_CONV - 2, :] = x_ref[...]


def conv_step(state, xbc, w, b):
    db, c = xbc.shape
    return pl.pallas_call(
        _conv_step_kernel,
        out_shape=(jax.ShapeDtypeStruct((db, c), F32), jax.ShapeDtypeStruct(state.shape, F32)),
        compiler_params=pltpu.CompilerParams(vmem_limit_bytes=VMEM_LIMIT),
        name="conv_step",
    )(state, xbc, w, b.reshape(1, c))


def _ssd_scan_kernel(x_ref, b_ref, c_ref, dt_ref, par_ref, y_ref, st_ref, s_ref):
    g = pl.program_id(1)
    c = pl.program_id(2)
    L = SSD_CHUNK
    hpg = SSD_HEADS_PER_GROUP
    n_pairs = hpg // 2

    @pl.when(c == 0)
    def _():
        s_ref[...] = jnp.zeros_like(s_ref)

    sel = ((_iota((LANES, LANES), 0) == g * hpg + _iota((LANES, LANES), 1))
           & (_iota((LANES, LANES), 1) < hpg)).astype(F32)
    dtg = jnp.dot(dt_ref[0], sel, precision=HIGHEST, preferred_element_type=F32)
    par = jnp.dot(par_ref[...], sel, precision=HIGHEST, preferred_element_type=F32)
    a_neg = -jnp.exp(par[0:1])
    d_skip = par[1:2]
    dt = _softplus(dtg + par[2:3])
    a = dt * a_neg
    tri = (_iota((L, L), 1) <= _iota((L, L), 0))
    a_cs = jnp.dot(tri.astype(F32), a, precision=HIGHEST, preferred_element_type=F32)
    a_cs_t = a_cs.T
    ea = jnp.exp(a_cs)
    ds = jnp.exp(a_cs[L - 1:L, :] - a_cs)

    bm = b_ref[0]
    cm = c_ref[0].astype(BF16)
    cb = lax.dot_general(cm, bm.astype(BF16), _NT, preferred_element_type=F32)
    bt = bm.T.astype(BF16)
    left = _iota((L, LANES), 1) < SSD_HEAD_DIM

    def pair_bcast(v, e0):
        r = v.shape[0]
        lft = left if r == L else left[0:r]
        return jnp.where(lft, jnp.broadcast_to(v[:, e0:e0 + 1], (r, LANES)),
                         jnp.broadcast_to(v[:, e0 + 1:e0 + 2], (r, LANES)))

    for pr in range(n_pairs):
        e0 = 2 * pr
        cols = slice(pr * LANES, (pr + 1) * LANES)
        x = x_ref[0, :, cols]
        xdt = x * pair_bcast(dt, e0)
        y = x * pair_bcast(d_skip, e0)
        for e in (e0, e0 + 1):
            diff = a_cs[:, e:e + 1] - a_cs_t[e:e + 1, :]
            w = (cb * jnp.where(tri, jnp.exp(diff), 0.0)).astype(BF16)
            half = left if e == e0 else jnp.logical_not(left)
            y = y + jnp.dot(w, jnp.where(half, xdt, 0.0).astype(BF16), preferred_element_type=F32)
        ea_p = pair_bcast(ea, e0)
        s_old = s_ref[:, cols]
        y = y + jnp.dot(cm, s_old.astype(BF16), preferred_element_type=F32) * ea_p
        y_ref[0, :, cols] = y
        xs = (xdt * pair_bcast(ds, e0)).astype(BF16)
        s_new = s_old * ea_p[L - 1:L, :] + jnp.dot(bt, xs, preferred_element_type=F32)
        s_ref[:, cols] = s_new

        @pl.when(c == pl.num_programs(2) - 1)
        def _():
            st_ref[0, cols, :] = s_new.T


def ssd_scan(xbc, dt_raw, params):
    bsz, t, _ = xbc.shape
    gw = SSD_HEADS_PER_GROUP * SSD_HEAD_DIM
    d_inner = SSD_GROUPS * gw
    L = SSD_CHUNK
    boff = d_inner // LANES
    coff = boff + SSD_GROUPS
    return pl.pallas_call(
        _ssd_scan_kernel,
        out_shape=(jax.ShapeDtypeStruct((bsz, t, d_inner), F32),
                   jax.ShapeDtypeStruct((bsz, d_inner, SSD_STATE), F32)),
        grid=(bsz, SSD_GROUPS, t // L),
        in_specs=[pl.BlockSpec((1, L, gw), lambda b, g, c: (b, c, g)),
                  pl.BlockSpec((1, L, SSD_STATE), lambda b, g, c: (b, c, boff + g)),
                  pl.BlockSpec((1, L, SSD_STATE), lambda b, g, c: (b, c, coff + g)),
                  pl.BlockSpec((1, L, LANES), lambda b, g, c: (b, c, 0)),
                  pl.BlockSpec((SUBLANES, LANES), lambda b, g, c: (0, 0))],
        out_specs=(pl.BlockSpec((1, L, gw), lambda b, g, c: (b, c, g)),
                   pl.BlockSpec((1, gw, SSD_STATE), lambda b, g, c: (b, g, 0))),
        scratch_shapes=[pltpu.VMEM((SSD_STATE, gw), F32)],
        compiler_params=_params("parallel", "parallel", "arbitrary"),
        name="ssd_scan",
    )(xbc, xbc, xbc, dt_raw, params)


def _ssd_step_kernel(h_ref, x_ref, dt_ref, bias_ref, alog_ref, dsk_ref, b_ref, c_ref, y_ref, hn_ref):
    dt = _softplus(dt_ref[0] + bias_ref[...])
    decay = jnp.exp(dt * -jnp.exp(alog_ref[...]))
    x = x_ref[0]
    h_new = h_ref[0] * decay + (x * dt) * b_ref[0]
    hn_ref[0] = h_new
    y_ref[0] = jnp.sum(h_new * c_ref[0], axis=1, keepdims=True) + dsk_ref[...] * x


def ssd_step(h0, x_col, dt_col, bias_col, alog_col, dsk_col, xbc_c):
    db, rows, n = h0.shape
    gw = SSD_HEADS_PER_GROUP * SSD_HEAD_DIM
    boff = rows // LANES
    coff = boff + SSD_GROUPS
    col = lambda: pl.BlockSpec((1, gw, 1), lambda b, g: (b, g, 0))
    pcol = lambda: pl.BlockSpec((gw, 1), lambda b, g: (g, 0))
    return pl.pallas_call(
        _ssd_step_kernel,
        out_shape=(jax.ShapeDtypeStruct((db, rows, 1), F32),
                   jax.ShapeDtypeStruct((db, rows, n), F32)),
        grid=(db, SSD_GROUPS),
        in_specs=[pl.BlockSpec((1, gw, n), lambda b, g: (b, g, 0)),
                  col(), col(), pcol(), pcol(), pcol(),
                  pl.BlockSpec((1, 1, n), lambda b, g: (b, 0, boff + g)),
                  pl.BlockSpec((1, 1, n), lambda b, g: (b, 0, coff + g))],
        out_specs=(pl.BlockSpec((1, gw, 1), lambda b, g: (b, g, 0)),
                   pl.BlockSpec((1, gw, n), lambda b, g: (b, g, 0))),
        compiler_params=_params("parallel", "parallel"),
        name="ssd_step",
    )(h0, x_col, dt_col, bias_col, alog_col, dsk_col, xbc_c, xbc_c)


def _gate_norm_kernel(y_ref, z_ref, w_ref, o_ref):
    v = y_ref[...] * _silu(z_ref[...])
    ms = jnp.mean(v * v, axis=-1, keepdims=True)
    o_ref[...] = (v * lax.rsqrt(ms + RMS_EPS) * w_ref[...]).astype(o_ref.dtype)


def gate_norm_bf16(y, z, w):
    m, d = y.shape
    tm = min(m, 128)
    return pl.pallas_call(
        _gate_norm_kernel,
        out_shape=jax.ShapeDtypeStruct((m, d), BF16),
        grid=(m // tm,),
        in_specs=[pl.BlockSpec((tm, d), lambda i: (i, 0)),
                  pl.BlockSpec((tm, d), lambda i: (i, 0)),
                  pl.BlockSpec((1, d), lambda i: (0, 0))],
        out_specs=pl.BlockSpec((tm, d), lambda i: (i, 0)),
        compiler_params=_params("parallel"),
        name="gate_norm_bf16",
    )(y, z, w.reshape(1, d))


def kernel(x_prompt, x_sample, cache_kv_sb, cache_kv_fox, cache_logf_fox, state_conv, state_ssm, page_table,
           norm_pre, norm_post, attn_w_in, attn_b_f, attn_w_out, ssd_w_in, ssd_conv_w, ssd_conv_b,
           ssd_dt_bias, ssd_A_log, ssd_D, ssd_norm_w, ssd_w_out):
    bp, t, d = x_prompt.shape
    db = x_sample.shape[0]
    h = N_HEADS
    hw = h * HEAD_DIM
    d_inner = ssd_norm_w.shape[1]
    conv_dim = ssd_conv_w.shape[2]
    n_ssd_heads = ssd_A_log.shape[1]
    mp = bp * t

    w_in = attn_w_in[0]
    w_q = jnp.concatenate([w_in[:, 0:hw], w_in[:, 3 * hw:4 * hw]], axis=1).astype(BF16)
    w_kv_sb = w_in[:, hw:3 * hw].astype(BF16)
    w_kv_fx = w_in[:, 4 * hw:6 * hw].astype(BF16)
    w_f = jnp.pad(w_in[:, 6 * hw:6 * hw + h], ((0, 0), (0, LANES - h))).astype(BF16)
    w_gate = w_in[:, 6 * hw + h:].astype(BF16)
    b_f = jnp.pad(attn_b_f[0], (0, LANES - h)).reshape(1, LANES)
    w_out0 = attn_w_out[0].astype(BF16)
    w_s = ssd_w_in[0]
    w_z = w_s[:, :d_inner].astype(BF16)
    w_xbc = w_s[:, d_inner:d_inner + conv_dim].astype(BF16)
    w_dt = w_s[:, d_inner + conv_dim:].astype(BF16)
    w_out1 = ssd_w_out[0].astype(BF16)
    ssd_params = jnp.zeros((SUBLANES, LANES), F32)
    ssd_params = ssd_params.at[0].set(ssd_A_log[0]).at[1].set(ssd_D[0]).at[2].set(ssd_dt_bias[0])

    def attn_proj(x2d):
        hb = rmsnorm_bf16(x2d, norm_pre[0])
        return (matmul(hb, w_q), matmul(hb, w_kv_sb), matmul(hb, w_kv_fx),
                matmul(hb, w_f), matmul(hb, w_gate))

    def ssd_proj(x2d):
        hb = rmsnorm_bf16(x2d, norm_pre[1])
        return matmul(hb, w_z), matmul(hb, w_xbc), matmul(hb, w_dt)

    xp = x_prompt.reshape(mp, d)
    q, kv_sb, kv_fx, fl, gate = attn_proj(xp)
    logf, csum, csum_t = logf_cumsum(fl.reshape(bp, t, LANES), b_f)
    nk = t // ATT_TK
    ck = csum_t[:, :h, :].reshape(bp, h * nk, 1, ATT_TK)
    a = prompt_attention(q.reshape(bp, t, 2 * hw), kv_sb.reshape(bp, t, 2 * hw), kv_fx.reshape(bp, t, 2 * hw),
                         csum, ck, gate.reshape(bp, t, 2 * hw))
    xp = proj_norm_residual(a.reshape(mp, 2 * hw), w_out0, norm_post[0], xp)

    z, xbc, dt_raw = ssd_proj(xp)
    xbc3 = xbc.reshape(bp, t, conv_dim)
    xbc_c = conv_prompt(xbc3, ssd_conv_w[0], ssd_conv_b[0])
    y, ssm_p = ssd_scan(xbc_c, dt_raw.reshape(bp, t, n_ssd_heads), ssd_params)
    a = gate_norm_bf16(y.reshape(mp, d_inner), z, ssd_norm_w[0])
    y_prompt = proj_norm_residual(a, w_out1, norm_post[1], xp).reshape(bp, t, d)

    kv_sb_prompt = kv_sb.reshape(1, bp, t, 2, h, HEAD_DIM)
    kv_fox_prompt = kv_fx.reshape(1, bp, t, 2, h, HEAD_DIM)
    logf_prompt = logf[:, :, :h].reshape(1, bp, t, h)
    conv_prompt_out = xbc3[:, t - (D_CONV - 1):, :].reshape(1, bp, D_CONV - 1, conv_dim)
    ssm_prompt = ssm_p.reshape(1, bp, n_ssd_heads, SSD_HEAD_DIM, SSD_STATE)

    xs = x_sample.reshape(db, d)
    q, kv_sb, kv_fx, fl, gate = attn_proj(xs)
    lf_new = logf_new(fl, b_f)
    n_pool, page = cache_kv_sb.shape[1], cache_kv_sb.shape[2]
    a_sb, a_fx = decode_attention(
        page_table, q.reshape(db, 1, 2 * hw), kv_fx.reshape(db, 1, 2 * hw), lf_new.reshape(db, 1, LANES),
        cache_kv_sb[0].reshape(n_pool, page, 2 * hw), cache_kv_fox[0].reshape(n_pool, page, 2 * hw),
        cache_logf_fox[0], gate.reshape(db, 1, 2 * hw))
    a = jnp.concatenate([a_sb, a_fx], axis=-1).reshape(db, 2 * hw)
    xs = proj_norm_residual(a, w_out0, norm_post[0], xs)

    z, xbc, dt_raw = ssd_proj(xs)
    xbc_c, conv_s = conv_step(state_conv[0], xbc, ssd_conv_w[0], ssd_conv_b[0])
    rep = lambda v: jnp.repeat(v, SSD_HEAD_DIM, axis=-1)[..., None]
    y_col, ssm_s = ssd_step(
        state_ssm[0].reshape(db, d_inner, SSD_STATE),
        xbc_c[:, :d_inner].reshape(db, d_inner, 1), rep(dt_raw),
        rep(ssd_dt_bias[0]), rep(ssd_A_log[0]), rep(ssd_D[0]),
        xbc_c.reshape(db, 1, conv_dim))
    a = gate_norm_bf16(y_col.reshape(db, d_inner), z, ssd_norm_w[0])
    y_sample = proj_norm_residual(a, w_out1, norm_post[1], xs).reshape(db, 1, d)

    kv_sb_sample = kv_sb.reshape(1, db, 1, 2, h, HEAD_DIM)
    kv_fox_sample = kv_fx.reshape(1, db, 1, 2, h, HEAD_DIM)
    logf_sample = lf_new[:, :h].reshape(1, db, 1, h)
    conv_sample = conv_s.reshape(1, db, D_CONV - 1, conv_dim)
    ssm_sample = ssm_s.reshape(1, db, n_ssd_heads, SSD_HEAD_DIM, SSD_STATE)

    return (y_prompt, y_sample, kv_sb_prompt, kv_sb_sample, kv_fox_prompt, kv_fox_sample,
            logf_prompt, logf_sample, conv_prompt_out, conv_sample, ssm_prompt, ssm_sample)
```

```python
import functools
import math

import jax
import jax.numpy as jnp
from jax import lax
from jax.experimental import pallas as pl
from jax.experimental.pallas import tpu as pltpu

F32 = jnp.float32
BF16 = jnp.bfloat16
HIGHEST = lax.Precision.HIGHEST

HEAD_DIM = 128
N_HEADS = 16
ATTN_SCALE = HEAD_DIM ** -0.5
LOG2E = math.log2(math.e)
SSD_HEAD_DIM = 64
SSD_GROUPS = 8
SSD_HEADS_PER_GROUP = 16
SSD_STATE = 128
SSD_CHUNK = 128
D_CONV = 4
RMS_EPS = 1e-6
LANES = 128
SUBLANES = 8
VMEM_LIMIT = 56 * 1024 * 1024
NEG_BIG = -1e30

_NT = (((1,), (1,)), ((), ()))


def _params(*sem):
    return pltpu.CompilerParams(dimension_semantics=sem, vmem_limit_bytes=VMEM_LIMIT)


def _softplus(x):
    return jnp.maximum(x, 0.0) + jnp.log(1.0 + jnp.exp(-jnp.abs(x)))


def _softplus2(x):
    return jnp.maximum(x, 0.0) + jnp.log2(1.0 + jnp.exp2(-jnp.abs(x)))


def _log_sigmoid(x):
    return jnp.minimum(x, 0.0) - jnp.log(1.0 + jnp.exp(-jnp.abs(x)))


def _silu(x):
    return x * (1.0 / (1.0 + jnp.exp(-x)))


def _split2(x):
    hi = x.astype(BF16)
    lo = (x - hi.astype(F32)).astype(BF16)
    return hi, lo


def _iota(shape, dim):
    return lax.broadcasted_iota(jnp.int32, shape, dim)


def _suffix_and_total():
    r = _iota((2 * LANES, 2 * LANES), 0) % LANES
    c = _iota((2 * LANES, 2 * LANES), 1)
    return ((r > c) | (c >= LANES)).astype(BF16)


def _rmsnorm_kernel(x_ref, w_ref, o_ref):
    x = x_ref[...]
    ms = jnp.mean(x * x, axis=-1, keepdims=True)
    o_ref[...] = (x * lax.rsqrt(ms + RMS_EPS) * w_ref[...]).astype(o_ref.dtype)


def rmsnorm_bf16(x, w):
    m, d = x.shape
    tm = min(m, 256)
    return pl.pallas_call(
        _rmsnorm_kernel,
        out_shape=jax.ShapeDtypeStruct((m, d), BF16),
        grid=(m // tm,),
        in_specs=[pl.BlockSpec((tm, d), lambda i: (i, 0)),
                  pl.BlockSpec((1, d), lambda i: (0, 0))],
        out_specs=pl.BlockSpec((tm, d), lambda i: (i, 0)),
        compiler_params=_params("parallel"),
        name="rmsnorm_bf16",
    )(x, w.reshape(1, d))


def _matmul_kernel(a_ref, w_ref, o_ref):
    o_ref[...] = jnp.dot(a_ref[...], w_ref[...], preferred_element_type=F32)


def matmul(a, w):
    m, k = a.shape
    n = w.shape[1]
    tm = min(m, 1024)
    tn = 512 if n % 512 == 0 else LANES
    return pl.pallas_call(
        _matmul_kernel,
        out_shape=jax.ShapeDtypeStruct((m, n), F32),
        grid=(m // tm, n // tn),
        in_specs=[pl.BlockSpec((tm, k), lambda i, j: (i, 0)),
                  pl.BlockSpec((k, tn), lambda i, j: (0, j))],
        out_specs=pl.BlockSpec((tm, tn), lambda i, j: (i, j)),
        compiler_params=_params("parallel", "arbitrary"),
        name="matmul",
    )(a, w)


def _proj_norm_res_kernel(a_ref, w_ref, nw_ref, x_ref, o_ref):
    k = pl.program_id(1)

    @pl.when(k == 0)
    def _():
        o_ref[...] = jnp.zeros_like(o_ref)

    o_ref[...] += jnp.dot(a_ref[...], w_ref[...], preferred_element_type=F32)

    @pl.when(k == pl.num_programs(1) - 1)
    def _():
        r = o_ref[...]
        ms = jnp.mean(r * r, axis=-1, keepdims=True)
        o_ref[...] = x_ref[...] + r * lax.rsqrt(ms + RMS_EPS) * nw_ref[...]


def proj_norm_residual(a, w, nw, x):
    m, kdim = a.shape
    d = w.shape[1]
    tm = min(m, 512)
    tk = 512
    return pl.pallas_call(
        _proj_norm_res_kernel,
        out_shape=jax.ShapeDtypeStruct((m, d), F32),
        grid=(m // tm, kdim // tk),
        in_specs=[pl.BlockSpec((tm, tk), lambda i, k: (i, k)),
                  pl.BlockSpec((tk, d), lambda i, k: (k, 0)),
                  pl.BlockSpec((1, d), lambda i, k: (0, 0)),
                  pl.BlockSpec((tm, d), lambda i, k: (i, 0))],
        out_specs=pl.BlockSpec((tm, d), lambda i, k: (i, 0)),
        compiler_params=_params("parallel", "arbitrary"),
        name="proj_norm_residual",
    )(a, w, nw.reshape(1, d), x)


def _logf_kernel(f_ref, b_ref, logf_ref, csum_ref, csumt_ref):
    t = f_ref.shape[1]
    tri = (_iota((LANES, LANES), 1) <= _iota((LANES, LANES), 0)).astype(F32)
    carry = jnp.zeros((1, LANES), F32)
    for c in range(t // LANES):
        rows = pl.ds(c * LANES, LANES)
        lf = _log_sigmoid(f_ref[0, rows, :] + b_ref[...])
        logf_ref[0, rows, :] = lf
        cs = jnp.dot(tri, lf, precision=HIGHEST, preferred_element_type=F32) + carry
        csum_ref[0, rows, :] = cs
        csumt_ref[0, :, rows] = cs.T
        carry = cs[LANES - 1:LANES, :]


def logf_cumsum(f, b):
    bsz, t, w = f.shape
    blk = pl.BlockSpec((1, t, w), lambda i: (i, 0, 0))
    return pl.pallas_call(
        _logf_kernel,
        out_shape=(jax.ShapeDtypeStruct((bsz, t, w), F32),
                   jax.ShapeDtypeStruct((bsz, t, w), F32),
                   jax.ShapeDtypeStruct((bsz, w, t), F32)),
        grid=(bsz,),
        in_specs=[blk, pl.BlockSpec((1, w), lambda i: (0, 0))],
        out_specs=(blk, blk, pl.BlockSpec((1, w, t), lambda i: (i, 0, 0))),
        compiler_params=_params("parallel"),
        name="logf_cumsum",
    )(f, b)


SB_TQ = 512
SB_SUB = LANES
FX_TQ = 512


def _sb_attn_kernel(q_ref, k_ref, v_ref, gate_ref, o_ref):
    qi = pl.program_id(2)
    tq = SB_TQ
    n_sub = tq // SB_SUB
    q = (q_ref[0] * (ATTN_SCALE * LOG2E)).astype(BF16)
    mask = _iota((tq, tq), 1) < _iota((tq, tq), 0)
    tmo = _suffix_and_total()

    def group(j, c_b, acc, masked):
        rows = pl.ds(pl.multiple_of(j * tq, tq), tq)
        kb = k_ref[0, rows, :].astype(BF16)
        vb = v_ref[0, rows, :].astype(BF16)
        z = lax.dot_general(q, kb, _NT, preferred_element_type=F32)
        sp = _softplus2(z)
        sp_m = jnp.where(mask, sp, 0.0) if masked else sp
        ws = [None] * n_sub
        for u in reversed(range(n_sub)):
            cols = slice(u * SB_SUB, (u + 1) * SB_SUB)
            hi, lo = _split2(sp_m[:, cols])
            lt = jnp.dot(jnp.concatenate([hi, lo], axis=1), tmo, preferred_element_type=F32)
            ws[u] = jnp.exp2(z[:, cols] - sp[:, cols] - lt[:, :SB_SUB] - c_b)
            c_b = c_b + lt[:, SB_SUB:]
        w = jnp.concatenate(ws, axis=1)
        if masked:
            w = jnp.where(mask, w, 0.0)
        acc = acc + jnp.dot(w.astype(BF16), vb, preferred_element_type=F32)
        return c_b, acc

    c_b = jnp.zeros((tq, SB_SUB), F32)
    acc = jnp.zeros((tq, HEAD_DIM), F32)
    c_b, acc = group(qi, c_b, acc, True)

    def body(it, ca):
        return group(qi - 1 - it, ca[0], ca[1], False)

    c_b, acc = lax.fori_loop(0, qi, body, (c_b, acc))
    o_ref[0] = (acc * _silu(gate_ref[0])).astype(o_ref.dtype)


def _fx_attn_kernel(q_ref, k_ref, v_ref, cq_ref, ck_ref, gate_ref, o_ref):
    hh = pl.program_id(1)
    qi = pl.program_id(2)
    tq = FX_TQ
    nk = ck_ref.shape[1] // N_HEADS
    q = (q_ref[0] * (ATTN_SCALE * LOG2E)).astype(BF16)
    cq = jnp.sum(jnp.where(_iota((tq, LANES), 1) == hh, cq_ref[0], 0.0), axis=1, keepdims=True) * LOG2E
    mask = _iota((tq, tq), 1) <= _iota((tq, tq), 0)

    def group(j, m, l, acc, masked):
        rows = pl.ds(pl.multiple_of(j * tq, tq), tq)
        kb = k_ref[0, rows, :].astype(BF16)
        vb = v_ref[0, rows, :].astype(BF16)
        s = lax.dot_general(q, kb, _NT, preferred_element_type=F32)
        s = s + cq - ck_ref[0, hh * nk + j] * LOG2E
        if masked:
            s = jnp.where(mask, s, NEG_BIG)
        m_new = jnp.maximum(m, jnp.max(s, axis=1, keepdims=True))
        p = jnp.exp2(s - m_new)
        if masked:
            p = jnp.where(mask, p, 0.0)
        alpha = jnp.exp2(m - m_new)
        l = alpha * l + jnp.sum(p, axis=1, keepdims=True)
        acc = alpha * acc + jnp.dot(p.astype(BF16), vb, preferred_element_type=F32)
        return m_new, l, acc

    def body(j, c):
        return group(j, c[0], c[1], c[2], False)

    m = jnp.full((tq, 1), NEG_BIG, F32)
    l = jnp.zeros((tq, 1), F32)
    acc = jnp.zeros((tq, HEAD_DIM), F32)
    m, l, acc = lax.fori_loop(0, qi, body, (m, l, acc))
    m, l, acc = group(qi, m, l, acc, True)
    o_ref[0] = ((acc / l) * _silu(gate_ref[0])).astype(o_ref.dtype)


def prompt_attention(q, kv_sb, kv_fx, csum, csum_t, gate):
    bsz, t, width = q.shape
    h = N_HEADS
    hw = h * HEAD_DIM
    kv_spec = lambda off: pl.BlockSpec((1, t, HEAD_DIM), lambda b, g, i: (b, 0, off + g))

    tq = SB_TQ
    blk = lambda off: pl.BlockSpec((1, tq, HEAD_DIM), lambda b, g, i: (b, i, off + g))
    a_sb = pl.pallas_call(
        _sb_attn_kernel,
        out_shape=jax.ShapeDtypeStruct((bsz, t, hw), BF16),
        grid=(bsz, h, t // tq),
        in_specs=[blk(0), kv_spec(0), kv_spec(h), blk(0)],
        out_specs=blk(0),
        compiler_params=_params("parallel", "parallel", "arbitrary"),
        name="sb_attention",
    )(q, kv_sb, kv_sb, gate)

    tq = FX_TQ
    nk = t // tq
    ck = csum_t[:, :h, :].reshape(bsz, h * nk, 1, tq)
    blk = lambda off: pl.BlockSpec((1, tq, HEAD_DIM), lambda b, g, i: (b, i, off + g))
    a_fx = pl.pallas_call(
        _fx_attn_kernel,
        out_shape=jax.ShapeDtypeStruct((bsz, t, hw), BF16),
        grid=(bsz, h, t // tq),
        in_specs=[blk(h), kv_spec(0), kv_spec(h),
                  pl.BlockSpec((1, tq, LANES), lambda b, g, i: (b, i, 0)),
                  pl.BlockSpec((1, h * nk, 1, tq), lambda b, g, i: (b, 0, 0, 0)),
                  blk(h)],
        out_specs=blk(0),
        compiler_params=_params("parallel", "parallel", "arbitrary"),
        name="fx_attention",
    )(q, kv_fx, kv_fx, csum, ck, gate)
    return jnp.concatenate([a_sb, a_fx], axis=-1)


DEC_PG = 4


def _head_diag():
    n = LANES * N_HEADS
    return (_iota((N_HEADS, n), 1) % N_HEADS) == _iota((N_HEADS, n), 0)


def _decode_scores_kernel(pt_ref, q_ref, *refs, with_lf):
    k_refs = refs[:DEC_PG]
    refs = refs[DEC_PG:]
    if with_lf:
        lf_refs, (z_ref, lfo_ref) = refs[:DEC_PG], refs[DEC_PG:]
    else:
        (z_ref,) = refs
    q = (q_ref[0] * ATTN_SCALE).astype(BF16)
    diag = _head_diag()
    for i in range(DEC_PG):
        kf = k_refs[i][...].reshape(LANES * N_HEADS, HEAD_DIM).astype(BF16)
        zt = lax.dot_general(q, kf, _NT, preferred_element_type=F32)
        z_ref[0, i] = jnp.sum(jnp.where(diag, zt, 0.0), axis=0, keepdims=True)
        if with_lf:
            lfo_ref[0, i] = lf_refs[i][...]


def _page_specs(n_pages, kv_index):
    def spec(i):
        return pl.BlockSpec((None, None, LANES, None, N_HEADS, HEAD_DIM),
                            lambda b, j, pt: (0, pt[b, j * DEC_PG + i], 0, kv_index, 0, 0))
    return [spec(i) for i in range(DEC_PG)]


def decode_scores(page_table, q32, kind, cache_kv, cache_lf=None):
    db, n_pages = page_table.shape
    n = LANES * N_HEADS
    with_lf = cache_lf is not None
    in_specs = [pl.BlockSpec((1, N_HEADS, HEAD_DIM), lambda b, j, pt: (b, kind, 0))] + _page_specs(n_pages, 0)
    out_shape = [jax.ShapeDtypeStruct((db, n_pages, 1, n), F32)]
    out_specs = [pl.BlockSpec((1, DEC_PG, 1, n), lambda b, j, pt: (b, j, 0, 0))]
    args = [page_table, q32] + [cache_kv] * DEC_PG
    if with_lf:
        in_specs += [pl.BlockSpec((None, None, LANES, N_HEADS),
                                  (lambda i: lambda b, j, pt: (0, pt[b, j * DEC_PG + i], 0, 0))(i))
                     for i in range(DEC_PG)]
        out_shape.append(jax.ShapeDtypeStruct((db, n_pages, LANES, N_HEADS), F32))
        out_specs.append(pl.BlockSpec((1, DEC_PG, LANES, N_HEADS), lambda b, j, pt: (b, j, 0, 0)))
        args += [cache_lf] * DEC_PG
    return pl.pallas_call(
        functools.partial(_decode_scores_kernel, with_lf=with_lf),
        out_shape=tuple(out_shape),
        grid_spec=pltpu.PrefetchScalarGridSpec(
            num_scalar_prefetch=1, grid=(db, n_pages // DEC_PG),
            in_specs=in_specs, out_specs=tuple(out_specs)),
        compiler_params=_params("parallel", "arbitrary"),
        name="decode_scores",
    )(*args)


def _later_pages_matrix(rows, n_pages):
    r = _iota((rows, rows), 0)
    c = _iota((rows, rows), 1)
    return ((r // n_pages == c // n_pages) & (c > r)).astype(BF16)


def _page_suffix(x, n_pages):
    hi, lo = _split2(x)
    lt = jnp.dot(jnp.concatenate([hi, lo], axis=1), _suffix_and_total(), preferred_element_type=F32)
    thi, tlo = _split2(lt[:, LANES:])
    ub = _later_pages_matrix(x.shape[0], n_pages)
    cross = jnp.dot(ub, thi, preferred_element_type=F32) + jnp.dot(ub, tlo, preferred_element_type=F32)
    return lt[:, :LANES] + cross


def _sb_weights_kernel(z_ref, w_ref, *, n_pages):
    z = z_ref[0]
    sp = _softplus(z)
    w_ref[0] = jnp.exp(z - sp - _page_suffix(sp, n_pages))


def _per_head(x, n_pages, op):
    r = op(x, axis=1, keepdims=True)
    r = op(r.reshape(N_HEADS, n_pages, 1), axis=1, keepdims=True)
    return jnp.broadcast_to(r, (N_HEADS, n_pages, 1)).reshape(N_HEADS * n_pages, 1)


def _fx_weights_kernel(z_ref, lf_ref, q_ref, kn_ref, lfn_ref, w_ref, ws_ref, *, n_pages):
    rows = N_HEADS * n_pages
    z = z_ref[0]
    lf = lf_ref[0]
    zn = jnp.sum(q_ref[0] * kn_ref[0], axis=1, keepdims=True) * ATTN_SCALE
    head_rows = (_iota((rows, LANES), 0) // n_pages == _iota((rows, LANES), 1)).astype(F32)
    pad = jnp.zeros((LANES - N_HEADS, LANES), F32)
    expand = lambda v: jnp.dot(head_rows, jnp.concatenate([jnp.broadcast_to(v, (N_HEADS, LANES)), pad], axis=0),
                               precision=HIGHEST, preferred_element_type=F32)[:, 0:1]
    zn_r = expand(zn)
    lfn_r = expand(lfn_ref[0])
    s = z + _page_suffix(lf, n_pages) + lfn_r
    m = jnp.maximum(_per_head(s, n_pages, jnp.max), zn_r)
    p = jnp.exp(s - m)
    p_new = jnp.exp(zn_r - m)
    l = _per_head(p, n_pages, jnp.sum) + p_new
    w_ref[0] = p / l
    ws_ref[0] = jnp.broadcast_to(p_new / l, (rows, LANES))


def decode_weights_sb(z_rows, n_pages):
    db, rows, _ = z_rows.shape
    blk = pl.BlockSpec((1, rows, LANES), lambda b: (b, 0, 0))
    return pl.pallas_call(
        functools.partial(_sb_weights_kernel, n_pages=n_pages),
        out_shape=jax.ShapeDtypeStruct((db, rows, LANES), F32),
        grid=(db,), in_specs=[blk], out_specs=blk,
        compiler_params=_params("parallel"),
        name="decode_weights_sb",
    )(z_rows)


def decode_weights_fx(z_rows, lf_rows, q32, kvn32, lfn, n_pages):
    db, rows, _ = z_rows.shape
    blk = pl.BlockSpec((1, rows, LANES), lambda b: (b, 0, 0))
    hblk = lambda k: pl.BlockSpec((1, N_HEADS, HEAD_DIM), lambda b: (b, k, 0))
    return pl.pallas_call(
        functools.partial(_fx_weights_kernel, n_pages=n_pages),
        out_shape=(jax.ShapeDtypeStruct((db, rows, LANES), F32),) * 2,
        grid=(db,),
        in_specs=[blk, blk, hblk(1), hblk(0), pl.BlockSpec((1, N_HEADS, 1), lambda b: (b, 0, 0))],
        out_specs=(blk, blk),
        compiler_params=_params("parallel"),
        name="decode_weights_fx",
    )(z_rows, lf_rows, q32, kvn32, lfn)


def _decode_values_kernel(pt_ref, w_ref, gate_ref, *refs, with_new):
    v_refs = refs[:DEC_PG]
    refs = refs[DEC_PG:]
    if with_new:
        wn_ref, vn_ref, o_ref, acc_ref = refs
    else:
        o_ref, acc_ref = refs
    j = pl.program_id(1)

    @pl.when(j == 0)
    def _():
        acc_ref[...] = jnp.zeros_like(acc_ref)

    diag = _head_diag()
    acc = acc_ref[...]
    for i in range(DEC_PG):
        vf = v_refs[i][...].reshape(LANES * N_HEADS, HEAD_DIM).astype(BF16)
        wm = jnp.where(diag, jnp.broadcast_to(w_ref[0, i], diag.shape), 0.0).astype(BF16)
        acc = acc + jnp.dot(wm, vf, preferred_element_type=F32)
    acc_ref[...] = acc

    @pl.when(j == pl.num_programs(1) - 1)
    def _():
        o = acc
        if with_new:
            o = o + wn_ref[0] * vn_ref[0]
        o_ref[0] = (o * _silu(gate_ref[0])).astype(o_ref.dtype)


def decode_values(page_table, w_flat, gate32, kind, cache_kv, w_new=None, kvn32=None):
    db, n_pages = page_table.shape
    n = LANES * N_HEADS
    with_new = w_new is not None
    hblk = lambda k: pl.BlockSpec((1, N_HEADS, HEAD_DIM), lambda b, j, pt: (b, k, 0))
    in_specs = [pl.BlockSpec((1, DEC_PG, 1, n), lambda b, j, pt: (b, j, 0, 0)), hblk(kind)] + _page_specs(n_pages, 1)
    args = [page_table, w_flat, gate32] + [cache_kv] * DEC_PG
    if with_new:
        in_specs += [hblk(0), hblk(1)]
        args += [w_new, kvn32]
    return pl.pallas_call(
        functools.partial(_decode_values_kernel, with_new=with_new),
        out_shape=jax.ShapeDtypeStruct((db, N_HEADS, HEAD_DIM), BF16),
        grid_spec=pltpu.PrefetchScalarGridSpec(
            num_scalar_prefetch=1, grid=(db, n_pages // DEC_PG),
            in_specs=in_specs, out_specs=hblk(0),
            scratch_shapes=[pltpu.VMEM((N_HEADS, HEAD_DIM), F32)]),
        compiler_params=_params("parallel", "arbitrary"),
        name="decode_values",
    )(*args)


def decode_attention(page_table, q, kv_fx_new, logf_new_, cache_kv_sb, cache_kv_fox, cache_logf_fox, gate):
    db, n_pages = page_table.shape
    h = N_HEADS
    rows = h * n_pages
    q32 = q.reshape(db, 2 * h, HEAD_DIM)
    gate32 = gate.reshape(db, 2 * h, HEAD_DIM)
    kvn32 = kv_fx_new.reshape(db, 2 * h, HEAD_DIM)
    to_rows = lambda x: x.reshape(db, n_pages, LANES, h).transpose(0, 3, 1, 2).reshape(db, rows, LANES)
    to_flat = lambda x: x.reshape(db, h, n_pages, LANES).transpose(0, 2, 3, 1).reshape(db, n_pages, 1, LANES * h)

    (z,) = decode_scores(page_table, q32, 0, cache_kv_sb)
    w = decode_weights_sb(to_rows(z), n_pages)
    a_sb = decode_values(page_table, to_flat(w), gate32, 0, cache_kv_sb)

    z, lf = decode_scores(page_table, q32, 1, cache_kv_fox, cache_logf_fox)
    w, w_new = decode_weights_fx(to_rows(z), to_rows(lf), q32, kvn32, logf_new_[:, :h].reshape(db, h, 1), n_pages)
    w_new = w_new.reshape(db, h, n_pages, LANES)[:, :, 0, :]
    a_fx = decode_values(page_table, to_flat(w), gate32, 1, cache_kv_fox, w_new, kvn32)
    return jnp.concatenate([a_sb.reshape(db, h * HEAD_DIM), a_fx.reshape(db, h * HEAD_DIM)], axis=-1)


def _logf_new_kernel(f_ref, b_ref, o_ref):
    o_ref[...] = _log_sigmoid(f_ref[...] + b_ref[...])


def logf_new(f, b):
    return pl.pallas_call(
        _logf_new_kernel,
        out_shape=jax.ShapeDtypeStruct(f.shape, F32),
        name="logf_new",
    )(f, b)


CONV_TT = 512
CONV_TC = 1024
CONV_HALO = SUBLANES


def _conv_prompt_kernel(x_ref, prev_ref, w_ref, b_ref, o_ref, buf_ref):
    ti = pl.program_id(1)
    tt = x_ref.shape[1]
    halo = prev_ref[0]
    buf_ref[0:CONV_HALO, :] = jnp.where(ti > 0, halo, jnp.zeros_like(halo))
    buf_ref[CONV_HALO:, :] = x_ref[0]
    acc = jnp.broadcast_to(b_ref[...], (tt, b_ref.shape[1]))
    for j in range(D_CONV):
        start = CONV_HALO - (D_CONV - 1) + j
        acc = acc + buf_ref[pl.ds(start, tt), :] * w_ref[j:j + 1, :]
    o_ref[0] = _silu(acc)


def conv_prompt(xbc, w, b):
    bsz, t, c = xbc.shape
    tt = min(t, CONV_TT)
    tc = CONV_TC
    per = tt // CONV_HALO
    return pl.pallas_call(
        _conv_prompt_kernel,
        out_shape=jax.ShapeDtypeStruct((bsz, t, c), F32),
        grid=(bsz, t // tt, c // tc),
        in_specs=[pl.BlockSpec((1, tt, tc), lambda bb, i, j: (bb, i, j)),
                  pl.BlockSpec((1, CONV_HALO, tc), lambda bb, i, j: (bb, jnp.maximum(i * per - 1, 0), j)),
                  pl.BlockSpec((D_CONV, tc), lambda bb, i, j: (0, j)),
                  pl.BlockSpec((1, tc), lambda bb, i, j: (0, j))],
        out_specs=pl.BlockSpec((1, tt, tc), lambda bb, i, j: (bb, i, j)),
        scratch_shapes=[pltpu.VMEM((tt + CONV_HALO, tc), F32)],
        compiler_params=_params("parallel", "parallel", "parallel"),
        name="conv_prompt",
    )(xbc, xbc, w, b.reshape(1, c))


def _conv_step_kernel(state_ref, x_ref, w_ref, b_ref, o_ref, new_ref):
    acc = b_ref[...] + x_ref[...] * w_ref[D_CONV - 1:D_CONV, :]
    for j in range(D_CONV - 1):
        acc = acc + state_ref[:, j, :] * w_ref[j:j + 1, :]
    o_ref[...] = _silu(acc)
    for j in range(D_CONV - 2):
        new_ref[:, j, :] = state_ref[:, j + 1, :]
    new_ref[:, D_CONV - 2, :] = x_ref[...]


def conv_step(state, xbc, w, b):
    db, c = xbc.shape
    return pl.pallas_call(
        _conv_step_kernel,
        out_shape=(jax.ShapeDtypeStruct((db, c), F32), jax.ShapeDtypeStruct(state.shape, F32)),
        compiler_params=pltpu.CompilerParams(vmem_limit_bytes=VMEM_LIMIT),
        name="conv_step",
    )(state, xbc, w, b.reshape(1, c))


def _bf16_parts(v, n):
    parts, rest = [], v
    for _ in range(n):
        p = rest.astype(BF16)
        parts.append(p)
        rest = rest - p.astype(F32)
    return parts


def _ssd_prep_kernel(dt_ref, par_ref, perm_a_ref, perm_x_ref, lpa_ref, lpx_ref, acst_ref):
    L = SSD_CHUNK
    dt = _softplus(dt_ref[0] + par_ref[2:3, :])
    a = dt * -jnp.exp(par_ref[0:1, :])
    tri = (_iota((L, L), 1) <= _iota((L, L), 0)).astype(BF16)
    cs3 = jnp.dot(tri, jnp.concatenate(_bf16_parts(a, 3), axis=1), preferred_element_type=F32)
    a_cs = cs3[:, 0:LANES] + cs3[:, LANES:2 * LANES] + cs3[:, 2 * LANES:3 * LANES]
    acst_ref[0] = a_cs.T
    dtds = dt * jnp.exp(a_cs[L - 1:L, :] - a_cs)
    lpa_ref[0] = jnp.dot(jnp.concatenate(_bf16_parts(a_cs, 3), axis=1), perm_a_ref[...],
                         preferred_element_type=F32).astype(BF16)
    lpx_ref[0] = jnp.dot(jnp.concatenate(_bf16_parts(dt, 2) + _bf16_parts(dtds, 2), axis=1), perm_x_ref[...],
                         preferred_element_type=F32).astype(BF16)


def _group_perm(n_parts):
    hpg = SSD_HEADS_PER_GROUP
    rows, cols = n_parts * LANES, SSD_GROUPS * LANES
    r, c = _iota((rows, cols), 0), _iota((rows, cols), 1)
    part, head = r // LANES, r % LANES
    return ((c // LANES == head // hpg) & (c % LANES == part * hpg + head % hpg)).astype(BF16)


def _lane_expander(part_lo, part_hi, width):
    hpg = SSD_HEADS_PER_GROUP
    r, c = _iota((LANES, hpg * width), 0), _iota((LANES, hpg * width), 1)
    return ((r % hpg == c // width) & (r >= part_lo * hpg) & (r < part_hi * hpg)).astype(BF16)


def _ssd_scan_kernel(x_ref, b_ref, c_ref, lpa_ref, lpx_ref, acst_ref, dsk_ref, ea_ref, ex_ref,
                     y_ref, st_ref, s_ref):
    c = pl.program_id(2)
    L = SSD_CHUNK
    gw = SSD_HEADS_PER_GROUP * SSD_HEAD_DIM
    n_pairs = SSD_HEADS_PER_GROUP // 2

    @pl.when(c == 0)
    def _():
        s_ref[...] = jnp.zeros_like(s_ref)

    acol = jnp.dot(lpa_ref[0], ea_ref[...], preferred_element_type=F32)
    xp = jnp.dot(lpx_ref[0], ex_ref[...], preferred_element_type=F32)
    a_cs_t = acst_ref[0]
    tri = _iota((L, L), 1) <= _iota((L, L), 0)
    bm = b_ref[0]
    cm = c_ref[0]
    cb = lax.dot_general(cm.astype(BF16), bm.astype(BF16), _NT, preferred_element_type=F32)
    bt = bm.T.astype(BF16)
    left = _iota((L, LANES), 1) < SSD_HEAD_DIM
    left_row = _iota((1, LANES), 1) < SSD_HEAD_DIM

    s_all = s_ref[...]
    s_next = []
    for pr in range(n_pairs):
        cols = slice(pr * LANES, (pr + 1) * LANES)
        x = x_ref[0, :, cols]
        xdt = x * xp[:, cols]
        s_old = s_all[:, cols]
        lhs, rhs, ea_last = [], [], []
        for e in (2 * pr, 2 * pr + 1):
            ac = acol[:, e * LANES:(e + 1) * LANES]
            dec = jnp.where(tri, jnp.exp(ac - a_cs_t[e:e + 1, :]), 0.0)
            ea = jnp.exp(ac)
            half = left if e == 2 * pr else jnp.logical_not(left)
            lhs += [(cb * dec).astype(BF16), (cm * ea).astype(BF16)]
            rhs += [jnp.where(half, xdt, 0.0).astype(BF16), jnp.where(half, s_old, 0.0).astype(BF16)]
            ea_last.append(ea[L - 1:L, :])
        y = x * dsk_ref[:, cols] + jnp.dot(jnp.concatenate(lhs, axis=1), jnp.concatenate(rhs, axis=0),
                                           preferred_element_type=F32)
        y_ref[0, :, cols] = y
        xs = (x * xp[:, gw + pr * LANES:gw + (pr + 1) * LANES]).astype(BF16)
        s_next.append(s_old * jnp.where(left_row, ea_last[0], ea_last[1])
                      + jnp.dot(bt, xs, preferred_element_type=F32))
    for pr in range(n_pairs):
        s_ref[:, pr * LANES:(pr + 1) * LANES] = s_next[pr]

    @pl.when(c == pl.num_programs(2) - 1)
    def _():
        for pr in range(n_pairs):
            cols = slice(pr * LANES, (pr + 1) * LANES)
            st_ref[0, cols, :] = s_ref[:, cols].T


def ssd_scan(xbc, dt_raw, params, d_skip_x):
    bsz, t, _ = xbc.shape
    hpg = SSD_HEADS_PER_GROUP
    gw = hpg * SSD_HEAD_DIM
    d_inner = SSD_GROUPS * gw
    L = SSD_CHUNK
    boff = d_inner // LANES
    coff = boff + SSD_GROUPS
    n_heads = dt_raw.shape[2]
    const = lambda shape: pl.BlockSpec(shape, lambda *_: (0,) * len(shape))

    lpa, lpx, acst = pl.pallas_call(
        _ssd_prep_kernel,
        out_shape=(jax.ShapeDtypeStruct((bsz, t, SSD_GROUPS * LANES), BF16),
                   jax.ShapeDtypeStruct((bsz, t, SSD_GROUPS * LANES), BF16),
                   jax.ShapeDtypeStruct((bsz, n_heads, t), F32)),
        grid=(bsz, t // L),
        in_specs=[pl.BlockSpec((1, L, n_heads), lambda b, c: (b, c, 0)),
                  const((SUBLANES, LANES)), const((3 * LANES, SSD_GROUPS * LANES)),
                  const((4 * LANES, SSD_GROUPS * LANES))],
        out_specs=(pl.BlockSpec((1, L, SSD_GROUPS * LANES), lambda b, c: (b, c, 0)),
                   pl.BlockSpec((1, L, SSD_GROUPS * LANES), lambda b, c: (b, c, 0)),
                   pl.BlockSpec((1, n_heads, L), lambda b, c: (b, 0, c))),
        compiler_params=_params("parallel", "parallel"),
        name="ssd_prep",
    )(dt_raw, params, _group_perm(3), _group_perm(4))

    expand_a = _lane_expander(0, 3, LANES)
    expand_x = jnp.concatenate([_lane_expander(0, 2, SSD_HEAD_DIM), _lane_expander(2, 4, SSD_HEAD_DIM)], axis=1)
    return pl.pallas_call(
        _ssd_scan_kernel,
        out_shape=(jax.ShapeDtypeStruct((bsz, t, d_inner), F32),
                   jax.ShapeDtypeStruct((bsz, d_inner, SSD_STATE), F32)),
        grid=(bsz, SSD_GROUPS, t // L),
        in_specs=[pl.BlockSpec((1, L, gw), lambda b, g, c: (b, c, g)),
                  pl.BlockSpec((1, L, SSD_STATE), lambda b, g, c: (b, c, boff + g)),
                  pl.BlockSpec((1, L, SSD_STATE), lambda b, g, c: (b, c, coff + g)),
                  pl.BlockSpec((1, L, LANES), lambda b, g, c: (b, c, g)),
                  pl.BlockSpec((1, L, LANES), lambda b, g, c: (b, c, g)),
                  pl.BlockSpec((1, hpg, L), lambda b, g, c: (b, g, c)),
                  pl.BlockSpec((1, gw), lambda b, g, c: (0, g)),
                  const((LANES, hpg * LANES)), const((LANES, 2 * gw))],
        out_specs=(pl.BlockSpec((1, L, gw), lambda b, g, c: (b, c, g)),
                   pl.BlockSpec((1, gw, SSD_STATE), lambda b, g, c: (b, g, 0))),
        scratch_shapes=[pltpu.VMEM((SSD_STATE, gw), F32)],
        compiler_params=_params("parallel", "parallel", "arbitrary"),
        name="ssd_scan",
    )(xbc, xbc, xbc, lpa, lpx, acst, d_skip_x, expand_a, expand_x)


def _ssd_step_kernel(h_ref, x_ref, dt_ref, bias_ref, alog_ref, dsk_ref, b_ref, c_ref, y_ref, hn_ref):
    dt = _softplus(dt_ref[0] + bias_ref[...])
    decay = jnp.exp(dt * -jnp.exp(alog_ref[...]))
    x = x_ref[0]
    h_new = h_ref[0] * decay + (x * dt) * b_ref[0]
    hn_ref[0] = h_new
    y_ref[0] = jnp.sum(h_new * c_ref[0], axis=1, keepdims=True) + dsk_ref[...] * x


def ssd_step(h0, x_col, dt_col, bias_col, alog_col, dsk_col, xbc_c):
    db, rows, n = h0.shape
    gw = SSD_HEADS_PER_GROUP * SSD_HEAD_DIM
    boff = rows // LANES
    coff = boff + SSD_GROUPS
    col = lambda: pl.BlockSpec((1, gw, 1), lambda b, g: (b, g, 0))
    pcol = lambda: pl.BlockSpec((gw, 1), lambda b, g: (g, 0))
    return pl.pallas_call(
        _ssd_step_kernel,
        out_shape=(jax.ShapeDtypeStruct((db, rows, 1), F32),
                   jax.ShapeDtypeStruct((db, rows, n), F32)),
        grid=(db, SSD_GROUPS),
        in_specs=[pl.BlockSpec((1, gw, n), lambda b, g: (b, g, 0)),
                  col(), col(), pcol(), pcol(), pcol(),
                  pl.BlockSpec((1, 1, n), lambda b, g: (b, 0, boff + g)),
                  pl.BlockSpec((1, 1, n), lambda b, g: (b, 0, coff + g))],
        out_specs=(pl.BlockSpec((1, gw, 1), lambda b, g: (b, g, 0)),
                   pl.BlockSpec((1, gw, n), lambda b, g: (b, g, 0))),
        compiler_params=_params("parallel", "parallel"),
        name="ssd_step",
    )(h0, x_col, dt_col, bias_col, alog_col, dsk_col, xbc_c, xbc_c)


def _gate_norm_kernel(y_ref, z_ref, w_ref, o_ref):
    v = y_ref[...] * _silu(z_ref[...])
    ms = jnp.mean(v * v, axis=-1, keepdims=True)
    o_ref[...] = (v * lax.rsqrt(ms + RMS_EPS) * w_ref[...]).astype(o_ref.dtype)


def gate_norm_bf16(y, z, w):
    m, d = y.shape
    tm = min(m, 128)
    return pl.pallas_call(
        _gate_norm_kernel,
        out_shape=jax.ShapeDtypeStruct((m, d), BF16),
        grid=(m // tm,),
        in_specs=[pl.BlockSpec((tm, d), lambda i: (i, 0)),
                  pl.BlockSpec((tm, d), lambda i: (i, 0)),
                  pl.BlockSpec((1, d), lambda i: (0, 0))],
        out_specs=pl.BlockSpec((tm, d), lambda i: (i, 0)),
        compiler_params=_params("parallel"),
        name="gate_norm_bf16",
    )(y, z, w.reshape(1, d))


def kernel(x_prompt, x_sample, cache_kv_sb, cache_kv_fox, cache_logf_fox, state_conv, state_ssm, page_table,
           norm_pre, norm_post, attn_w_in, attn_b_f, attn_w_out, ssd_w_in, ssd_conv_w, ssd_conv_b,
           ssd_dt_bias, ssd_A_log, ssd_D, ssd_norm_w, ssd_w_out):
    bp, t, d = x_prompt.shape
    db = x_sample.shape[0]
    h = N_HEADS
    hw = h * HEAD_DIM
    d_inner = ssd_norm_w.shape[1]
    conv_dim = ssd_conv_w.shape[2]
    n_ssd_heads = ssd_A_log.shape[1]
    mp = bp * t

    w_in = attn_w_in[0]
    w_q = jnp.concatenate([w_in[:, 0:hw], w_in[:, 3 * hw:4 * hw]], axis=1).astype(BF16)
    w_kv_sb = w_in[:, hw:3 * hw].astype(BF16)
    w_kv_fx = w_in[:, 4 * hw:6 * hw].astype(BF16)
    w_f = jnp.pad(w_in[:, 6 * hw:6 * hw + h], ((0, 0), (0, LANES - h))).astype(BF16)
    w_gate = w_in[:, 6 * hw + h:].astype(BF16)
    b_f = jnp.pad(attn_b_f[0], (0, LANES - h)).reshape(1, LANES)
    w_out0 = attn_w_out[0].astype(BF16)
    w_s = ssd_w_in[0]
    w_z = w_s[:, :d_inner].astype(BF16)
    w_xbc = w_s[:, d_inner:d_inner + conv_dim].astype(BF16)
    w_dt = w_s[:, d_inner + conv_dim:].astype(BF16)
    w_out1 = ssd_w_out[0].astype(BF16)
    ssd_params = jnp.zeros((SUBLANES, LANES), F32)
    ssd_params = ssd_params.at[0].set(ssd_A_log[0]).at[1].set(ssd_D[0]).at[2].set(ssd_dt_bias[0])

    def attn_proj(x2d):
        hb = rmsnorm_bf16(x2d, norm_pre[0])
        return (matmul(hb, w_q), matmul(hb, w_kv_sb), matmul(hb, w_kv_fx),
                matmul(hb, w_f), matmul(hb, w_gate))

    def ssd_proj(x2d):
        hb = rmsnorm_bf16(x2d, norm_pre[1])
        return matmul(hb, w_z), matmul(hb, w_xbc), matmul(hb, w_dt)

    xp = x_prompt.reshape(mp, d)
    q, kv_sb, kv_fx, fl, gate = attn_proj(xp)
    logf, csum, csum_t = logf_cumsum(fl.reshape(bp, t, LANES), b_f)
    a = prompt_attention(q.reshape(bp, t, 2 * hw), kv_sb.reshape(bp, t, 2 * hw), kv_fx.reshape(bp, t, 2 * hw),
                         csum, csum_t, gate.reshape(bp, t, 2 * hw))
    xp = proj_norm_residual(a.reshape(mp, 2 * hw), w_out0, norm_post[0], xp)

    z, xbc, dt_raw = ssd_proj(xp)
    xbc3 = xbc.reshape(bp, t, conv_dim)
    xbc_c = conv_prompt(xbc3, ssd_conv_w[0], ssd_conv_b[0])
    d_skip_x = jnp.repeat(ssd_D[0], SSD_HEAD_DIM).reshape(1, d_inner)
    y, ssm_p = ssd_scan(xbc_c, dt_raw.reshape(bp, t, n_ssd_heads), ssd_params, d_skip_x)
    a = gate_norm_bf16(y.reshape(mp, d_inner), z, ssd_norm_w[0])
    y_prompt = proj_norm_residual(a, w_out1, norm_post[1], xp).reshape(bp, t, d)

    kv_sb_prompt = kv_sb.reshape(1, bp, t, 2, h, HEAD_DIM)
    kv_fox_prompt = kv_fx.reshape(1, bp, t, 2, h, HEAD_DIM)
    logf_prompt = logf[:, :, :h].reshape(1, bp, t, h)
    conv_prompt_out = xbc3[:, t - (D_CONV - 1):, :].reshape(1, bp, D_CONV - 1, conv_dim)
    ssm_prompt = ssm_p.reshape(1, bp, n_ssd_heads, SSD_HEAD_DIM, SSD_STATE)

    xs = x_sample.reshape(db, d)
    q, kv_sb, kv_fx, fl, gate = attn_proj(xs)
    lf_new = logf_new(fl, b_f)
    a = decode_attention(page_table, q, kv_fx, lf_new, cache_kv_sb, cache_kv_fox, cache_logf_fox, gate)
    xs = proj_norm_residual(a, w_out0, norm_post[0], xs)

    z, xbc, dt_raw = ssd_proj(xs)
    xbc_c, conv_s = conv_step(state_conv[0], xbc, ssd_conv_w[0], ssd_conv_b[0])
    rep = lambda v: jnp.repeat(v, SSD_HEAD_DIM, axis=-1)[..., None]
    y_col, ssm_s = ssd_step(
        state_ssm[0].reshape(db, d_inner, SSD_STATE),
        xbc_c[:, :d_inner].reshape(db, d_inner, 1), rep(dt_raw),
        rep(ssd_dt_bias[0]), rep(ssd_A_log[0]), rep(ssd_D[0]),
        xbc_c.reshape(db, 1, conv_dim))
    a = gate_norm_bf16(y_col.reshape(db, d_inner), z, ssd_norm_w[0])
    y_sample = proj_norm_residual(a, w_out1, norm_post[1], xs).reshape(db, 1, d)

    kv_sb_sample = kv_sb.reshape(1, db, 1, 2, h, HEAD_DIM)
    kv_fox_sample = kv_fx.reshape(1, db, 1, 2, h, HEAD_DIM)
    logf_sample = lf_new[:, :h].reshape(1, db, 1, h)
    conv_sample = conv_s.reshape(1, db, D_CONV - 1, conv_dim)
    ssm_sample = ssm_s.reshape(1, db, n_ssd_heads, SSD_HEAD_DIM, SSD_STATE)

    return (y_prompt, y_sample, kv_sb_prompt, kv_sb_sample, kv_fox_prompt, kv_fox_sample,
            logf_prompt, logf_sample, conv_prompt_out, conv_sample, ssm_prompt, ssm_sample)
```
